```python
import jax, jax.numpy as jnp
from jax import lax
import numpy as np

D_MODEL = 1024
BATCH = 8
SEQ = 4096
DEPTH = 4

CHUNK = 64
HEAD_DIM = 64
SB_HEADS = 8
CA_HEADS = 8
LEFT_CHUNKS = 8
REL_CLIP = 256
QBLOCK = 128
SB_W = SB_HEADS * HEAD_DIM
CA_W = CA_HEADS * HEAD_DIM
MIX_WIDTH = SB_W + CA_W
ML_HEADS = 4
ML_QK_DIM = 128
ML_V_DIM = D_MODEL // ML_HEADS
ML_QK_W = ML_HEADS * ML_QK_DIM
ML_V_W = ML_HEADS * ML_V_DIM
ML_IN_W = 2 * ML_QK_W + 2 * ML_V_W + 2 * ML_HEADS
CONV_K = 4
GATE_SOFTCAP = 15.0
D_FF = 2816
N_EXPERTS = 8
TOP_K = 2
D_FF_EXPERT = 3584
N_EVEN = (DEPTH + 1) // 2
N_ODD = DEPTH // 2
EPS = 1e-6

kernel_name = "chunk_causal_hybrid_sb_band_mlstm_moe"


def rmsnorm(x, g):
    xf = x.astype(jnp.float32)
    y = xf * lax.rsqrt(jnp.mean(xf * xf, axis=-1, keepdims=True) + EPS)
    return (y * g.astype(jnp.float32)).astype(x.dtype)


def split_heads(t, n_heads):
    b, s, _ = t.shape
    return t.reshape(b, s, n_heads, -1).transpose(0, 2, 1, 3)


def swiglu(x, wg, wu, wd):
    return (jax.nn.silu(x @ wg) * (x @ wu)) @ wd


def stick_breaking_attention(q, k, v):
    seq, d = q.shape[2], q.shape[3]
    scale = d ** -0.5
    outs = []
    for blk in range(seq // QBLOCK):
        q0 = blk * QBLOCK
        kend = q0 + QBLOCK
        qb = q[:, :, q0:kend].astype(jnp.float32)
        kb = k[:, :, :kend].astype(jnp.float32)
        vb = v[:, :, :kend].astype(jnp.float32)
        z = jnp.einsum('bhqd,bhkd->bhqk', qb, kb) * scale
        t_pos = q0 + jnp.arange(QBLOCK)[:, None]
        s_pos = jnp.arange(kend)[None, :]
        valid = s_pos < t_pos
        log_1mb = jnp.where(valid, jax.nn.log_sigmoid(-z), 0.0)
        between = lax.cumsum(log_1mb, axis=3, reverse=True) - log_1mb
        w = jnp.where(valid, jnp.exp(jax.nn.log_sigmoid(z) + between), 0.0)
        outs.append(jnp.einsum('bhqk,bhkd->bhqd', w, vb))
    return jnp.concatenate(outs, axis=2).astype(v.dtype)


def chunked_band_attention(q, k, v, rel_bias):
    b, h, seq, d = q.shape
    n_chunks = seq // CHUNK
    pad = LEFT_CHUNKS * CHUNK
    band = (LEFT_CHUNKS + 1) * CHUNK
    kp = jnp.pad(k, ((0, 0), (0, 0), (pad, 0), (0, 0)))
    vp = jnp.pad(v, ((0, 0), (0, 0), (pad, 0), (0, 0)))
    tq = jnp.arange(CHUNK)[:, None]
    jk = jnp.arange(band)[None, :]
    rel = tq + pad - jk
    bias = rel_bias[:, jnp.clip(rel, -REL_CLIP, REL_CLIP) + REL_CLIP].astype(jnp.float32)
    scale = d ** -0.5

    def one_chunk(c):
        qc = lax.dynamic_slice_in_dim(q, c * CHUNK, CHUNK, axis=2).astype(jnp.float32)
        kc = lax.dynamic_slice_in_dim(kp, c * CHUNK, band, axis=2).astype(jnp.float32)
        vc = lax.dynamic_slice_in_dim(vp, c * CHUNK, band, axis=2).astype(jnp.float32)
        s = jnp.einsum('bhqd,bhkd->bhqk', qc, kc) * scale + bias
        valid = jk >= pad - c * CHUNK
        p = jax.nn.softmax(jnp.where(valid, s, -jnp.inf), axis=-1)
        return jnp.einsum('bhqk,bhkd->bhqd', p, vc)

    out = lax.map(one_chunk, jnp.arange(n_chunks))
    return out.transpose(1, 2, 0, 3, 4).reshape(b, h, seq, d).astype(v.dtype)


def even_mixer(xn, w_in, rel_bias, w_out):
    b, s, _ = xn.shape
    u = xn @ w_in
    qa, ka, va, qb, kb, vb = jnp.split(
        u, [SB_W, 2 * SB_W, 3 * SB_W, 3 * SB_W + CA_W, 3 * SB_W + 2 * CA_W], axis=-1)
    oa = stick_breaking_attention(split_heads(qa, SB_HEADS), split_heads(ka, SB_HEADS),
                                  split_heads(va, SB_HEADS))
    ob = chunked_band_attention(split_heads(qb, CA_HEADS), split_heads(kb, CA_HEADS),
                                split_heads(vb, CA_HEADS), rel_bias)
    o = jnp.concatenate([oa, ob], axis=1)
    o = o.transpose(0, 2, 1, 3).reshape(b, s, MIX_WIDTH)
    return o @ w_out


def causal_depthwise_conv(x, w, bias):
    c = x.shape[-1]
    y = lax.conv_general_dilated(
        x, w.reshape(CONV_K, 1, c), window_strides=(1,), padding=[(CONV_K - 1, 0)],
        dimension_numbers=('NWC', 'WIO', 'NWC'), feature_group_count=c)
    return y + bias


def mlstm_chunk_step(carry, inp):
    c_mat, n_vec, m_prev = carry
    q, k, v, log_i, log_f = inp
    L = q.shape[2]
    causal = jnp.arange(L)[:, None] >= jnp.arange(L)[None, :]
    bcum = jnp.cumsum(log_f, axis=-1)
    dmat = jnp.where(causal, bcum[..., :, None] - bcum[..., None, :] + log_i[..., None, :], -jnp.inf)
    inter = bcum + m_prev[..., None]
    m_t = jnp.maximum(inter, jnp.max(dmat, axis=-1))
    scores = jnp.einsum('bhtd,bhsd->bhts', q, k) * jnp.exp(dmat - m_t[..., None])
    inter_scale = jnp.exp(inter - m_t)
    num = (jnp.einsum('bhts,bhsv->bhtv', scores, v)
           + inter_scale[..., None] * jnp.einsum('bhtd,bhdv->bhtv', q, c_mat))
    den = jnp.sum(scores, axis=-1) + inter_scale * jnp.einsum('bhtd,bhd->bht', q, n_vec)
    h = num / jnp.maximum(jnp.abs(den), jnp.exp(-m_t))[..., None]
    g = bcum[..., -1]
    decay_s = g[..., None] - bcum + log_i
    m_new = jnp.maximum(g + m_prev, jnp.max(decay_s, axis=-1))
    ws = jnp.exp(decay_s - m_new[..., None])
    carry_scale = jnp.exp(g + m_prev - m_new)
    c_new = carry_scale[..., None, None] * c_mat + jnp.einsum('bhs,bhsd,bhsv->bhdv', ws, k, v)
    n_new = carry_scale[..., None] * n_vec + jnp.einsum('bhs,bhsd->bhd', ws, k)
    return (c_new, n_new, m_new), h


def mlstm_mixer(xn, w_in, conv_w, conv_b, b_i, b_f, mh_g, w_out):
    b, s, _ = xn.shape
    n_chunks = s // CHUNK
    u = xn @ w_in
    qk, v, o_pre, gates = jnp.split(u, [2 * ML_QK_W, 2 * ML_QK_W + ML_V_W, 2 * ML_QK_W + 2 * ML_V_W], axis=-1)
    qk = jax.nn.silu(causal_depthwise_conv(qk, conv_w, conv_b))
    q, k = jnp.split(qk.astype(jnp.float32), 2, axis=-1)
    k = k * (ML_QK_DIM ** -0.5)
    gf = gates.astype(jnp.float32)
    i_pre = GATE_SOFTCAP * jnp.tanh((gf[..., :ML_HEADS] + b_i) / GATE_SOFTCAP)
    f_pre = GATE_SOFTCAP * jnp.tanh((gf[..., ML_HEADS:] + b_f) / GATE_SOFTCAP)
    log_i = i_pre
    log_f = jax.nn.log_sigmoid(f_pre)

    def to_chunks(t):
        return t.reshape(b, n_chunks, CHUNK, ML_HEADS, -1).transpose(1, 0, 3, 2, 4)

    def gate_chunks(t):
        return t.reshape(b, n_chunks, CHUNK, ML_HEADS).transpose(1, 0, 3, 2)

    xs = (to_chunks(q.reshape(b, s, ML_HEADS, ML_QK_DIM)),
          to_chunks(k.reshape(b, s, ML_HEADS, ML_QK_DIM)),
          to_chunks(v.astype(jnp.float32).reshape(b, s, ML_HEADS, ML_V_DIM)),
          gate_chunks(log_i), gate_chunks(log_f))
    init = (jnp.zeros((b, ML_HEADS, ML_QK_DIM, ML_V_DIM), jnp.float32),
            jnp.zeros((b, ML_HEADS, ML_QK_DIM), jnp.float32),
            jnp.zeros((b, ML_HEADS), jnp.float32))
    _, h = lax.scan(mlstm_chunk_step, init, xs)
    h = h.transpose(1, 0, 3, 2, 4).reshape(b, s, ML_HEADS, ML_V_DIM)
    h = h * lax.rsqrt(jnp.mean(h * h, axis=-1, keepdims=True) + EPS)
    h = h * mh_g.astype(jnp.float32).reshape(ML_HEADS, ML_V_DIM)
    h = (jax.nn.sigmoid(o_pre.astype(jnp.float32)) * h.reshape(b, s, ML_V_W)).astype(xn.dtype)
    return h @ w_out


def moe_swiglu(xn, w_router, wg, wu, wd):
    b, s, d = xn.shape
    xt = xn.reshape(b * s, d)
    logits = (xt @ w_router).astype(jnp.float32)
    top_val, top_idx = lax.top_k(logits, TOP_K)
    top_w = jax.nn.softmax(top_val, axis=-1)
    gate = jnp.sum(jax.nn.one_hot(top_idx, N_EXPERTS, dtype=jnp.float32) * top_w[..., None], axis=1)
    y = jnp.zeros_like(xt)
    for e in range(N_EXPERTS):
        y = y + gate[:, e:e + 1].astype(xt.dtype) * swiglu(xt, wg[e], wu[e], wd[e])
    return y.reshape(b, s, d)


def setup_inputs(seed: int = 0) -> dict:
    key = jax.random.key(seed)
    ks = jax.random.split(key, 24)
    f32 = jnp.float32

    def nrm(k, shape, fan_in):
        return jax.random.normal(k, shape, f32) * (fan_in ** -0.5)

    def gain(k, shape):
        return 1.0 + 0.02 * jax.random.normal(k, shape, f32)

    return {
        "x": jax.random.normal(ks[0], (BATCH, SEQ, D_MODEL), f32),
        "mix_norm_even": gain(ks[1], (N_EVEN, D_MODEL)),
        "w_in_attn": nrm(ks[2], (N_EVEN, D_MODEL, 3 * MIX_WIDTH), D_MODEL),
        "rel_bias": 0.2 * jax.random.normal(ks[3], (N_EVEN, CA_HEADS, 2 * REL_CLIP + 1), f32),
        "w_out_attn": nrm(ks[4], (N_EVEN, MIX_WIDTH, D_MODEL), MIX_WIDTH),
        "ffn_norm_even": gain(ks[5], (N_EVEN, D_MODEL)),
        "w_dense_gate": nrm(ks[6], (N_EVEN, D_MODEL, D_FF), D_MODEL),
        "w_dense_up": nrm(ks[7], (N_EVEN, D_MODEL, D_FF), D_MODEL),
        "w_dense_down": nrm(ks[8], (N_EVEN, D_FF, D_MODEL), D_FF),
        "mix_norm_odd": gain(ks[9], (N_ODD, D_MODEL)),
        "w_in_mlstm": nrm(ks[10], (N_ODD, D_MODEL, ML_IN_W), D_MODEL),
        "conv_w": nrm(ks[11], (N_ODD, CONV_K, 2 * ML_QK_W), CONV_K),
        "conv_b": 0.02 * jax.random.normal(ks[12], (N_ODD, 2 * ML_QK_W), f32),
        "b_igate": 0.1 * jax.random.normal(ks[13], (N_ODD, ML_HEADS), f32),
        "b_fgate": jnp.linspace(3.0, 6.0, ML_HEADS, dtype=f32)[None, :]
                   + 0.1 * jax.random.normal(ks[14], (N_ODD, ML_HEADS), f32),
        "mh_norm_g": gain(ks[15], (N_ODD, ML_V_W)),
        "w_out_mlstm": nrm(ks[16], (N_ODD, ML_V_W, D_MODEL), ML_V_W),
        "ffn_norm_odd": gain(ks[17], (N_ODD, D_MODEL)),
        "w_router": nrm(ks[18], (N_ODD, D_MODEL, N_EXPERTS), D_MODEL),
        "w_exp_gate": nrm(ks[19], (N_ODD, N_EXPERTS, D_MODEL, D_FF_EXPERT), D_MODEL),
        "w_exp_up": nrm(ks[20], (N_ODD, N_EXPERTS, D_MODEL, D_FF_EXPERT), D_MODEL),
        "w_exp_down": nrm(ks[21], (N_ODD, N_EXPERTS, D_FF_EXPERT, D_MODEL), D_FF_EXPERT),
        "final_norm_g": gain(ks[22], (D_MODEL,)),
    }


def reference(x, mix_norm_even, w_in_attn, rel_bias, w_out_attn, ffn_norm_even,
              w_dense_gate, w_dense_up, w_dense_down, mix_norm_odd, w_in_mlstm,
              conv_w, conv_b, b_igate, b_fgate, mh_norm_g, w_out_mlstm, ffn_norm_odd,
              w_router, w_exp_gate, w_exp_up, w_exp_down, final_norm_g):
    h = x
    for layer in range(DEPTH):
        j = layer // 2
        if layer % 2 == 0:
            h = h + even_mixer(rmsnorm(h, mix_norm_even[j]), w_in_attn[j], rel_bias[j], w_out_attn[j])
            h = h + swiglu(rmsnorm(h, ffn_norm_even[j]), w_dense_gate[j], w_dense_up[j], w_dense_down[j])
        else:
            h = h + mlstm_mixer(rmsnorm(h, mix_norm_odd[j]), w_in_mlstm[j], conv_w[j], conv_b[j],
                                b_igate[j], b_fgate[j], mh_norm_g[j], w_out_mlstm[j])
            h = h + moe_swiglu(rmsnorm(h, ffn_norm_odd[j]), w_router[j], w_exp_gate[j],
                               w_exp_up[j], w_exp_down[j])
    return rmsnorm(h, final_norm_g)
```

```python
import functools

import jax
import jax.numpy as jnp
from jax import lax
from jax.experimental import pallas as pl
from jax.experimental.pallas import tpu as pltpu

F32 = jnp.float32
BF16 = jnp.bfloat16

D_MODEL = 1024
HEAD_DIM = 64
CHUNK = 64
LEFT_CHUNKS = 8
REL_CLIP = 256
ML_HEADS = 4
ML_QK_DIM = 128
ML_V_DIM = 256
CONV_K = 4
GATE_SOFTCAP = 15.0
N_EXPERTS = 8
EPS = 1e-6
NEG = -1e30

VMEM_LIMIT = 48 * 1024 * 1024

TM_PROJ = 512
TF_DENSE = 1408
TM_MOE = 512
TF_MOE = 1792
TQ_SB = 256
TQ_CA = 256
WIN_CA = TQ_CA + LEFT_CHUNKS * CHUNK
L_ML = 128
TC_COMB = 256


def _cparams(sem):
    return pltpu.CompilerParams(dimension_semantics=sem, vmem_limit_bytes=VMEM_LIMIT)


def _rms(x, g):
    return x * lax.rsqrt(jnp.mean(x * x, axis=-1, keepdims=True) + EPS) * g


def _dot(a, b):
    return jnp.dot(a, b, preferred_element_type=F32)


def _dot_nt(a, b):
    return lax.dot_general(a, b, (((1,), (1,)), ((), ())), preferred_element_type=F32)


def _dot_tn(a, b):
    return lax.dot_general(a, b, (((0,), (0,)), ((), ())), preferred_element_type=F32)


def _split3(x):
    hi = x.astype(BF16)
    r = x - hi.astype(F32)
    mid = r.astype(BF16)
    lo = (r - mid.astype(F32)).astype(BF16)
    return hi, mid, lo


def _log_sigmoid(x):
    return jnp.minimum(x, 0.0) - jnp.log(1.0 + jnp.exp(-jnp.abs(x)))


def _norm_proj_kernel(x_ref, g_ref, w_ref, o_ref):
    xn = _rms(x_ref[...], g_ref[...]).astype(BF16)
    o_ref[...] = _dot(xn, w_ref[...]).astype(o_ref.dtype)


def _norm_proj_gates_kernel(x_ref, g_ref, w_ref, wgt_ref, o_ref, gt_ref):
    xn = _rms(x_ref[...], g_ref[...]).astype(BF16)
    o_ref[...] = _dot(xn, w_ref[...]).astype(o_ref.dtype)
    gt_ref[...] = _dot_nt(wgt_ref[...], xn)


def norm_proj(x, g, w, wgt=None):
    t, d = x.shape
    n = w.shape[1]
    tm = TM_PROJ
    in_specs = [
        pl.BlockSpec((tm, d), lambda i: (i, 0)),
        pl.BlockSpec((1, d), lambda i: (0, 0)),
        pl.BlockSpec((d, n), lambda i: (0, 0)),
    ]
    if wgt is None:
        return pl.pallas_call(
            _norm_proj_kernel,
            out_shape=jax.ShapeDtypeStruct((t, n), BF16),
            grid=(t // tm,),
            in_specs=in_specs,
            out_specs=pl.BlockSpec((tm, n), lambda i: (i, 0)),
            compiler_params=_cparams(("parallel",)),
            name="norm_proj",
        )(x, g, w)
    ng = wgt.shape[0]
    return pl.pallas_call(
        _norm_proj_gates_kernel,
        out_shape=(jax.ShapeDtypeStruct((t, n), BF16), jax.ShapeDtypeStruct((ng, t), F32)),
        grid=(t // tm,),
        in_specs=in_specs + [pl.BlockSpec((ng, d), lambda i: (0, 0))],
        out_specs=(pl.BlockSpec((tm, n), lambda i: (i, 0)), pl.BlockSpec((ng, tm), lambda i: (0, i))),
        compiler_params=_cparams(("parallel",)),
        name="norm_proj_gates",
    )(x, g, w, wgt)


def _proj_res_kernel(a_ref, w_ref, h_ref, o_ref):
    o_ref[...] = h_ref[...] + _dot(a_ref[...], w_ref[...])


def proj_res(a, w, h):
    t, k = a.shape
    n = w.shape[1]
    tm = TM_PROJ
    return pl.pallas_call(
        _proj_res_kernel,
        out_shape=jax.ShapeDtypeStruct((t, n), F32),
        grid=(t // tm,),
        in_specs=[
            pl.BlockSpec((tm, k), lambda i: (i, 0)),
            pl.BlockSpec((k, n), lambda i: (0, 0)),
            pl.BlockSpec((tm, n), lambda i: (i, 0)),
        ],
        out_specs=pl.BlockSpec((tm, n), lambda i: (i, 0)),
        compiler_params=_cparams(("parallel",)),
        name="proj_res",
    )(a, w, h)


def _ffn_kernel(x_ref, g_ref, wg_ref, wu_ref, wd_ref, o_ref, xn_ref, acc_ref):
    f = pl.program_id(1)

    @pl.when(f == 0)
    def _():
        x = x_ref[...]
        xn_ref[...] = _rms(x, g_ref[...]).astype(BF16)
        acc_ref[...] = x

    xn = xn_ref[...]
    a = _dot(xn, wg_ref[...])
    b = _dot(xn, wu_ref[...])
    mid = (a * jax.nn.sigmoid(a) * b).astype(BF16)
    acc_ref[...] += _dot(mid, wd_ref[...])

    @pl.when(f == pl.num_programs(1) - 1)
    def _():
        o_ref[...] = acc_ref[...]


def dense_ffn(h, g, wg, wu, wd):
    t, d = h.shape
    ff = wg.shape[1]
    tm, tf = TM_PROJ, TF_DENSE
    return pl.pallas_call(
        _ffn_kernel,
        out_shape=jax.ShapeDtypeStruct((t, d), F32),
        grid=(t // tm, ff // tf),
        in_specs=[
            pl.BlockSpec((tm, d), lambda i, f: (i, 0)),
            pl.BlockSpec((1, d), lambda i, f: (0, 0)),
            pl.BlockSpec((d, tf), lambda i, f: (0, f)),
            pl.BlockSpec((d, tf), lambda i, f: (0, f)),
            pl.BlockSpec((tf, d), lambda i, f: (f, 0)),
        ],
        out_specs=pl.BlockSpec((tm, d), lambda i, f: (i, 0)),
        scratch_shapes=[pltpu.VMEM((tm, d), BF16), pltpu.VMEM((tm, d), F32)],
        compiler_params=_cparams(("parallel", "arbitrary")),
        name="dense_ffn",
    )(h, g, wg, wu, wd)


def _sb_kernel(q_ref, k_ref, v_ref, o_ref, *, tq):
    i = pl.program_id(2)
    row = lax.broadcasted_iota(jnp.int32, (tq, tq), 0)
    col = lax.broadcasted_iota(jnp.int32, (tq, tq), 1)
    later = (row > col).astype(BF16)
    valid = col < row
    outs = []
    for hh in range(2):
        sl = slice(hh * HEAD_DIM, (hh + 1) * HEAD_DIM)
        q = q_ref[:, sl] * jnp.asarray(HEAD_DIM ** -0.5, BF16)

        def block(j, carry, masked, q=q, sl=sl):
            acc, run = carry
            start = pl.multiple_of(j * tq, tq)
            k = k_ref[pl.ds(start, tq), sl]
            v = v_ref[pl.ds(start, tq), sl]
            z = _dot_nt(q, k)
            lg = jnp.minimum(-z, 0.0) - jnp.log(1.0 + jnp.exp(-jnp.abs(z)))
            if masked:
                lg = jnp.where(valid, lg, 0.0)
            lg_hi = lg.astype(BF16)
            lg_lo = (lg - lg_hi.astype(F32)).astype(BF16)
            suffix = _dot(lg_hi, later) + _dot(lg_lo, later)
            w = jnp.exp(lg + z + (suffix + run))
            if masked:
                w = jnp.where(valid, w, 0.0)
            acc = acc + _dot(w.astype(BF16), v)
            run = run + suffix[:, 0:1] + lg[:, 0:1]
            return acc, run

        carry = (jnp.zeros((tq, HEAD_DIM), F32), jnp.zeros((tq, 1), F32))
        carry = block(i, carry, True)
        carry = lax.fori_loop(0, i, lambda jj, c: block(i - 1 - jj, c, False), carry)
        outs.append(carry[0])
    o_ref[...] = jnp.concatenate(outs, axis=1).astype(o_ref.dtype)


def sb_attention(u, batch, seq):
    t = u.shape[0]
    tq = TQ_SB
    nq = seq // tq
    pairs = 4
    return pl.pallas_call(
        functools.partial(_sb_kernel, tq=tq),
        out_shape=jax.ShapeDtypeStruct((t, pairs * 128), BF16),
        grid=(batch, pairs, nq),
        in_specs=[
            pl.BlockSpec((tq, 128), lambda b, p, i: (b * nq + i, p)),
            pl.BlockSpec((seq, 128), lambda b, p, i: (b, pairs + p)),
            pl.BlockSpec((seq, 128), lambda b, p, i: (b, 2 * pairs + p)),
        ],
        out_specs=pl.BlockSpec((tq, 128), lambda b, p, i: (b * nq + i, p)),
        compiler_params=_cparams(("parallel", "parallel", "arbitrary")),
        name="sb_attention",
    )(u, u, u)


def _ca_kernel(q_ref, k_ref, v_ref, b_ref, o_ref, *, tq, win):
    i = pl.program_id(2)
    start = pl.multiple_of(jnp.maximum(i * tq - LEFT_CHUNKS * CHUNK, 0), tq)
    outs = []
    for hh in range(2):
        sl = slice(hh * HEAD_DIM, (hh + 1) * HEAD_DIM)
        q = q_ref[:, sl] * jnp.asarray(HEAD_DIM ** -0.5, BF16)
        k = k_ref[pl.ds(start, win), sl]
        v = v_ref[pl.ds(start, win), sl]
        s = _dot_nt(q, k) + b_ref[0, hh]
        m = jnp.max(s, axis=-1, keepdims=True)
        p = jnp.exp(s - m)
        den = jnp.sum(p, axis=-1, keepdims=True)
        outs.append(_dot(p.astype(BF16), v) / den)
    o_ref[...] = jnp.concatenate(outs, axis=1).astype(o_ref.dtype)


def _ca_bias_tables(rel_bias):
    pad = LEFT_CHUNKS * CHUNK
    qi = jnp.arange(TQ_CA)[:, None]
    tables = []
    for shift in (pad, pad - TQ_CA, 0):
        kj = jnp.arange(WIN_CA)[None, :] + shift
        rel = qi + pad - kj
        kc = kj // CHUNK - LEFT_CHUNKS
        qc = qi // CHUNK
        ok = (kc <= qc) & (kc >= qc - LEFT_CHUNKS)
        b = rel_bias[:, jnp.clip(rel, -REL_CLIP, REL_CLIP) + REL_CLIP].astype(F32)
        tables.append(jnp.where(ok[None], b, NEG))
    return jnp.stack(tables)


def ca_attention(u, bias_tables, batch, seq):
    t = u.shape[0]
    tq, win = TQ_CA, WIN_CA
    nq = seq // tq
    pairs = 4
    base = 3 * pairs
    return pl.pallas_call(
        functools.partial(_ca_kernel, tq=tq, win=win),
        out_shape=jax.ShapeDtypeStruct((t, pairs * 128), BF16),
        grid=(batch, pairs, nq),
        in_specs=[
            pl.BlockSpec((tq, 128), lambda b, p, i: (b * nq + i, base + p)),
            pl.BlockSpec((seq, 128), lambda b, p, i: (b, base + pairs + p)),
            pl.BlockSpec((seq, 128), lambda b, p, i: (b, base + 2 * pairs + p)),
            pl.BlockSpec((1, 2, tq, win), lambda b, p, i: (jnp.minimum(i, 2), p, 0, 0)),
        ],
        out_specs=pl.BlockSpec((tq, 128), lambda b, p, i: (b * nq + i, p)),
        compiler_params=_cparams(("parallel", "parallel", "arbitrary")),
        name="ca_attention",
    )(u, u, u, bias_tables)


def _mlstm_kernel(qk_ref, v_ref, op_ref, gt_ref, cw_ref, cb_ref, bif_ref, mg_ref, o_ref,
                  cs_ref, c_ref, n_ref, m_ref, *, L):
    c = pl.program_id(1)
    halo = 8

    @pl.when(c == 0)
    def _():
        cs_ref[0:halo, :] = jnp.zeros((halo, cs_ref.shape[1]), F32)
        c_ref[...] = jnp.zeros(c_ref.shape, F32)
        n_ref[...] = jnp.zeros(n_ref.shape, F32)
        m_ref[...] = jnp.zeros(m_ref.shape, F32)

    cs_ref[halo:halo + L, :] = qk_ref[...].astype(F32)
    y = cb_ref[...]
    for j in range(CONV_K):
        off = halo - (CONV_K - 1) + j
        y = y + cw_ref[j:j + 1, :] * cs_ref[off:off + L, :]
    cs_ref[0:halo, :] = cs_ref[L:L + halo, :]
    y = y * jax.nn.sigmoid(y)

    pre = GATE_SOFTCAP * jnp.tanh((gt_ref[...] + bif_ref[...]) / GATE_SOFTCAP)
    log_i = pre[0:ML_HEADS]
    log_f = _log_sigmoid(pre[ML_HEADS:2 * ML_HEADS])
    row = lax.broadcasted_iota(jnp.int32, (L, L), 0)
    col = lax.broadcasted_iota(jnp.int32, (L, L), 1)
    upper = (row <= col).astype(BF16)
    eye = (row == col).astype(BF16)
    causal = row >= col
    bcum = sum(_dot(part, upper) for part in _split3(log_f))
    xr = jnp.concatenate([log_i, bcum], axis=0)
    xc = sum(_dot_nt(eye, part) for part in _split3(xr))

    qw = ML_HEADS * ML_QK_DIM
    for h in range(ML_HEADS):
        q = y[:, h * ML_QK_DIM:(h + 1) * ML_QK_DIM]
        k = y[:, qw + h * ML_QK_DIM:qw + (h + 1) * ML_QK_DIM] * (ML_QK_DIM ** -0.5)
        qb = q.astype(BF16)
        vsl = slice(h * ML_V_DIM, (h + 1) * ML_V_DIM)
        v = v_ref[:, vsl]
        bc = xc[:, ML_HEADS + h:ML_HEADS + h + 1]
        lic = xc[:, h:h + 1]
        r_row = xr[h:h + 1] - xr[ML_HEADS + h:ML_HEADS + h + 1]
        m_prev = m_ref[h:h + 1, 0:1]
        dm = jnp.where(causal, bc + r_row, NEG)
        inter = bc + m_prev
        m_t = jnp.maximum(inter, jnp.max(dm, axis=-1, keepdims=True))
        s = _dot_nt(qb, k.astype(BF16)) * jnp.exp(dm - m_t)
        isc = jnp.exp(inter - m_t)
        num = _dot(s.astype(BF16), v) + isc * _dot(qb, c_ref[h].astype(BF16))
        den = (jnp.sum(s, axis=-1, keepdims=True)
               + isc * jnp.sum(q * n_ref[h:h + 1, :], axis=-1, keepdims=True))
        hv = num / jnp.maximum(jnp.abs(den), jnp.exp(-m_t))
        hn = hv * lax.rsqrt(jnp.mean(hv * hv, axis=-1, keepdims=True) + EPS) * mg_ref[:, vsl]
        o_ref[:, vsl] = (jax.nn.sigmoid(op_ref[:, vsl].astype(F32)) * hn).astype(o_ref.dtype)

        g = xr[ML_HEADS + h:ML_HEADS + h + 1, L - 1:L]
        m_new = jnp.maximum(g + m_prev, jnp.max(g + r_row, axis=-1, keepdims=True))
        kw = k * jnp.exp(g + (lic - bc) - m_new)
        carry_scale = jnp.exp(g + m_prev - m_new)
        c_ref[h] = carry_scale * c_ref[h] + _dot_tn(kw.astype(BF16), v)
        n_ref[h:h + 1, :] = carry_scale * n_ref[h:h + 1, :] + jnp.sum(kw, axis=0, keepdims=True)
        m_ref[h:h + 1, :] = jnp.broadcast_to(m_new, (1, m_ref.shape[1]))


def mlstm_core(u, gt, conv_w, conv_b, b_if, mh_g, batch, seq):
    t = u.shape[0]
    L = L_ML
    nc = seq // L
    w = D_MODEL
    return pl.pallas_call(
        functools.partial(_mlstm_kernel, L=L),
        out_shape=jax.ShapeDtypeStruct((t, w), BF16),
        grid=(batch, nc),
        in_specs=[
            pl.BlockSpec((L, w), lambda b, c: (b * nc + c, 0)),
            pl.BlockSpec((L, w), lambda b, c: (b * nc + c, 1)),
            pl.BlockSpec((L, w), lambda b, c: (b * nc + c, 2)),
            pl.BlockSpec((2 * ML_HEADS, L), lambda b, c: (0, b * nc + c)),
            pl.BlockSpec((CONV_K, w), lambda b, c: (0, 0)),
            pl.BlockSpec((1, w), lambda b, c: (0, 0)),
            pl.BlockSpec((2 * ML_HEADS, 1), lambda b, c: (0, 0)),
            pl.BlockSpec((1, w), lambda b, c: (0, 0)),
        ],
        out_specs=pl.BlockSpec((L, w), lambda b, c: (b * nc + c, 0)),
        scratch_shapes=[
            pltpu.VMEM((L + 16, w), F32),
            pltpu.VMEM((ML_HEADS, ML_QK_DIM, ML_V_DIM), F32),
            pltpu.VMEM((8, ML_QK_DIM), F32),
            pltpu.VMEM((8, 128), F32),
        ],
        compiler_params=_cparams(("parallel", "arbitrary")),
        name="mlstm_core",
    )(u, u, u, gt, conv_w, conv_b, b_if, mh_g)


def _router_kernel(x_ref, g_ref, wrt_ref, idx_ref, wt_ref):
    xn = _rms(x_ref[...], g_ref[...]).astype(BF16)
    lg = _dot_nt(wrt_ref[...], xn)
    eid = lax.broadcasted_iota(jnp.int32, lg.shape, 0)
    m1 = jnp.max(lg, axis=0, keepdims=True)
    i1 = jnp.min(jnp.where(lg == m1, eid, N_EXPERTS), axis=0, keepdims=True)
    lg2 = jnp.where(eid == i1, -jnp.inf, lg)
    m2 = jnp.max(lg2, axis=0, keepdims=True)
    i2 = jnp.min(jnp.where(lg2 == m2, eid, N_EXPERTS), axis=0, keepdims=True)
    e = jnp.exp(m2 - m1)
    idx_ref[...] = jnp.concatenate([i1, i2], axis=0)
    wt_ref[...] = jnp.concatenate([1.0 / (1.0 + e), e / (1.0 + e)], axis=0)


def router(h, g, wrt):
    t, d = h.shape
    tm = TM_PROJ
    return pl.pallas_call(
        _router_kernel,
        out_shape=(jax.ShapeDtypeStruct((2, t), jnp.int32), jax.ShapeDtypeStruct((2, t), F32)),
        grid=(t // tm,),
        in_specs=[
            pl.BlockSpec((tm, d), lambda i: (i, 0)),
            pl.BlockSpec((1, d), lambda i: (0, 0)),
            pl.BlockSpec((N_EXPERTS, d), lambda i: (0, 0)),
        ],
        out_specs=(pl.BlockSpec((2, tm), lambda i: (0, i)), pl.BlockSpec((2, tm), lambda i: (0, i))),
        compiler_params=_cparams(("parallel",)),
        name="router",
    )(h, g, wrt)


def _row_gather_start(src_hbm, idx_smem, islot, dst, sem, n_rows):
    def body(r, carry):
        tok = idx_smem[islot, r]
        pltpu.make_async_copy(src_hbm.at[pl.ds(tok, 1), :], dst.at[pl.ds(r, 1), :], sem).start()
        return carry
    lax.fori_loop(0, n_rows, body, 0, unroll=8)


def _row_gather_wait(src_hbm, dst, sem, n_rows):
    pltpu.make_async_copy(src_hbm.at[pl.ds(0, n_rows), :], dst, sem).wait()


def _moe_ffn_kernel(te_ref, tv_ref, src_ref, h_ref, g_ref, wg_ref, wu_ref, wd_ref, o_ref,
                    idx_ref, rows_ref, xn_ref, acc_ref, isem, rsem, *, tm, n_tiles):
    i = pl.program_id(0)
    f = pl.program_id(1)
    slot = i % 2

    def idx_copy(tile, s):
        return pltpu.make_async_copy(src_ref.at[tile], idx_ref.at[s], isem.at[s])

    @pl.when(f == 0)
    def _():
        @pl.when(i == 0)
        def _():
            idx_copy(0, 0).start()
            idx_copy(0, 0).wait()
            _row_gather_start(h_ref, idx_ref, 0, rows_ref.at[0], rsem.at[0], tm)
            if n_tiles > 1:
                idx_copy(1, 1).start()

        _row_gather_wait(h_ref, rows_ref.at[slot], rsem.at[slot], tm)

        @pl.when(i + 1 < n_tiles)
        def _():
            idx_copy(i + 1, 1 - slot).wait()
            _row_gather_start(h_ref, idx_ref, 1 - slot, rows_ref.at[1 - slot], rsem.at[1 - slot], tm)

        @pl.when(i + 2 < n_tiles)
        def _():
            idx_copy(i + 2, slot).start()

        xn_ref[...] = _rms(rows_ref[slot], g_ref[...]).astype(BF16)
        acc_ref[...] = jnp.zeros(acc_ref.shape, F32)

    @pl.when(tv_ref[i] > 0)
    def _():
        xn = xn_ref[...]
        a = _dot(xn, wg_ref[0])
        b = _dot(xn, wu_ref[0])
        mid = (a * jax.nn.sigmoid(a) * b).astype(BF16)
        acc_ref[...] += _dot(mid, wd_ref[0])

    @pl.when(f == pl.num_programs(1) - 1)
    def _():
        o_ref[...] = acc_ref[...]


def moe_ffn(h, g, wg, wu, wd, tile_expert, tile_valid, src_rows):
    t, d = h.shape
    ff = wg.shape[2]
    tm, tf = TM_MOE, TF_MOE
    n_tiles = src_rows.shape[0]
    nf = ff // tf
    grid_spec = pltpu.PrefetchScalarGridSpec(
        num_scalar_prefetch=2,
        grid=(n_tiles, nf),
        in_specs=[
            pl.BlockSpec(memory_space=pl.ANY),
            pl.BlockSpec(memory_space=pl.ANY),
            pl.BlockSpec((1, d), lambda i, f, te, tv: (0, 0)),
            pl.BlockSpec((1, d, tf), lambda i, f, te, tv: (te[i], 0, f)),
            pl.BlockSpec((1, d, tf), lambda i, f, te, tv: (te[i], 0, f)),
            pl.BlockSpec((1, tf, d), lambda i, f, te, tv: (te[i], f, 0)),
        ],
        out_specs=pl.BlockSpec((tm, d), lambda i, f, te, tv: (i, 0)),
        scratch_shapes=[
            pltpu.SMEM((2, tm), jnp.int32),
            pltpu.VMEM((2, tm, d), F32),
            pltpu.VMEM((tm, d), BF16),
            pltpu.VMEM((tm, d), F32),
            pltpu.SemaphoreType.DMA((2,)),
            pltpu.SemaphoreType.DMA((2,)),
        ],
    )
    return pl.pallas_call(
        functools.partial(_moe_ffn_kernel, tm=tm, n_tiles=n_tiles),
        out_shape=jax.ShapeDtypeStruct((n_tiles * tm, d), F32),
        grid_spec=grid_spec,
        compiler_params=_cparams(("arbitrary", "arbitrary")),
        name="moe_ffn",
    )(tile_expert, tile_valid, src_rows, h, g, wg, wu, wd)


def _combine_kernel(pos_ref, ys_ref, h_ref, wt_ref, g_ref, o_ref, idx_ref, rows_ref, isem, rsem,
                    *, tc, n_tiles, final_norm):
    i = pl.program_id(0)
    slot = i % 2
    n_rows = 2 * tc

    def idx_copy(tile, s):
        return pltpu.make_async_copy(pos_ref.at[tile], idx_ref.at[s], isem.at[s])

    @pl.when(i == 0)
    def _():
        idx_copy(0, 0).start()
        idx_copy(0, 0).wait()
        _row_gather_start(ys_ref, idx_ref, 0, rows_ref.at[0], rsem.at[0], n_rows)
        if n_tiles > 1:
            idx_copy(1, 1).start()

    _row_gather_wait(ys_ref, rows_ref.at[slot], rsem.at[slot], n_rows)

    @pl.when(i + 1 < n_tiles)
    def _():
        idx_copy(i + 1, 1 - slot).wait()
        _row_gather_start(ys_ref, idx_ref, 1 - slot, rows_ref.at[1 - slot], rsem.at[1 - slot], n_rows)

    @pl.when(i + 2 < n_tiles)
    def _():
        idx_copy(i + 2, slot).start()

    wt = wt_ref[...]
    y = (h_ref[...] + wt[:, 0:1] * rows_ref[slot, 0:tc, :] + wt[:, 1:2] * rows_ref[slot, tc:n_rows, :])
    if final_norm:
        y = _rms(y, g_ref[...])
    o_ref[...] = y


def moe_combine(ys, h, pos_tiles, wt, g, final_norm):
    t, d = h.shape
    tc = TC_COMB
    n_tiles = t // tc
    return pl.pallas_call(
        functools.partial(_combine_kernel, tc=tc, n_tiles=n_tiles, final_norm=final_norm),
        out_shape=jax.ShapeDtypeStruct((t, d), F32),
        grid=(n_tiles,),
        in_specs=[
            pl.BlockSpec(memory_space=pl.ANY),
            pl.BlockSpec(memory_space=pl.ANY),
            pl.BlockSpec((tc, d), lambda i: (i, 0)),
            pl.BlockSpec((tc, 2), lambda i: (i, 0)),
            pl.BlockSpec((1, d), lambda i: (0, 0)),
        ],
        out_specs=pl.BlockSpec((tc, d), lambda i: (i, 0)),
        scratch_shapes=[
            pltpu.SMEM((2, 2 * tc), jnp.int32),
            pltpu.VMEM((2, 2 * tc, d), F32),
            pltpu.SemaphoreType.DMA((2,)),
            pltpu.SemaphoreType.DMA((2,)),
        ],
        compiler_params=_cparams(("arbitrary",)),
        name="moe_combine",
    )(pos_tiles, ys, h, wt, g)


def _routing_tables(idx, t):
    tm = TM_MOE
    n_tiles = (2 * t) // tm + N_EXPERTS
    e_flat = idx.reshape(2 * t)
    onehot = (e_flat[:, None] == jnp.arange(N_EXPERTS, dtype=jnp.int32)[None, :]).astype(jnp.int32)
    csum = jnp.cumsum(onehot, axis=0)
    counts = csum[-1]
    rank = jnp.sum(csum * onehot, axis=1) - 1
    tiles_per = (counts + tm - 1) // tm
    tile_end = jnp.cumsum(tiles_per)
    offsets = (tile_end - tiles_per) * tm
    pos = offsets[e_flat] + rank
    tok = jnp.tile(jnp.arange(t, dtype=jnp.int32), 2)
    src = jnp.zeros((n_tiles * tm,), jnp.int32).at[pos].set(tok)
    tile_ids = jnp.arange(n_tiles, dtype=jnp.int32)
    tile_expert = jnp.minimum(jnp.sum(tile_ids[:, None] >= tile_end[None, :], axis=1), N_EXPERTS - 1)
    tile_valid = (tile_ids < tile_end[-1]).astype(jnp.int32)
    return (tile_expert.astype(jnp.int32), tile_valid, src.reshape(n_tiles, tm),
            pos.astype(jnp.int32).reshape(2, t))


def moe_block(h, g, wrt, wg, wu, wd, final_g=None):
    t, d = h.shape
    idx, wts = router(h, g, wrt)
    tile_expert, tile_valid, src, pos = _routing_tables(idx, t)
    ys = moe_ffn(h, g, wg, wu, wd, tile_expert, tile_valid, src)
    tc = TC_COMB
    pos_tiles = pos.reshape(2, t // tc, tc).transpose(1, 0, 2).reshape(t // tc, 2 * tc)
    gn = final_g if final_g is not None else g
    return moe_combine(ys, h, pos_tiles, wts.T, gn, final_g is not None)


def kernel(x, mix_norm_even, w_in_attn, rel_bias, w_out_attn, ffn_norm_even, w_dense_gate, w_dense_up, w_dense_down, mix_norm_odd, w_in_mlstm, conv_w, conv_b, b_igate, b_fgate, mh_norm_g, w_out_mlstm, ffn_norm_odd, w_router, w_exp_gate, w_exp_up, w_exp_down, final_norm_g):
    batch, seq, d = x.shape
    t = batch * seq
    depth = mix_norm_even.shape[0] + mix_norm_odd.shape[0]
    h = x.reshape(t, d)
    main_w = 3 * D_MODEL
    for layer in range(depth):
        j = layer // 2
        if layer % 2 == 0:
            u = norm_proj(h, mix_norm_even[j][None], w_in_attn[j].astype(BF16))
            oa = sb_attention(u, batch, seq)
            ob = ca_attention(u, _ca_bias_tables(rel_bias[j]), batch, seq)
            o = jnp.concatenate([oa, ob], axis=1)
            h = proj_res(o, w_out_attn[j].astype(BF16), h)
            h = dense_ffn(h, ffn_norm_even[j][None], w_dense_gate[j].astype(BF16),
                          w_dense_up[j].astype(BF16), w_dense_down[j].astype(BF16))
        else:
            w_in = w_in_mlstm[j]
            u, gt = norm_proj(h, mix_norm_odd[j][None], w_in[:, :main_w].astype(BF16),
                              w_in[:, main_w:].T.astype(BF16))
            b_if = jnp.concatenate([b_igate[j], b_fgate[j]])[:, None]
            hm = mlstm_core(u, gt, conv_w[j], conv_b[j][None], b_if, mh_norm_g[j][None], batch, seq)
            h = proj_res(hm, w_out_mlstm[j].astype(BF16), h)
            last = layer == depth - 1
            h = moe_block(h, ffn_norm_odd[j][None], w_router[j].T.astype(BF16),
                          w_exp_gate[j].astype(BF16), w_exp_up[j].astype(BF16),
                          w_exp_down[j].astype(BF16), final_norm_g[None] if last else None)
    if depth % 2 == 1:
        raise NotImplementedError("final norm is fused into the last MoE combine")
    return h.reshape(batch, seq, d)
```

```python
import functools

import jax
import jax.numpy as jnp
from jax import lax
from jax.experimental import pallas as pl
from jax.experimental.pallas import tpu as pltpu

F32 = jnp.float32
BF16 = jnp.bfloat16

D_MODEL = 1024
HEAD_DIM = 64
CHUNK = 64
LEFT_CHUNKS = 8
REL_CLIP = 256
ML_HEADS = 4
ML_QK_DIM = 128
ML_V_DIM = 256
CONV_K = 4
GATE_SOFTCAP = 15.0
N_EXPERTS = 8
EPS = 1e-6
NEG = -1e30

VMEM_LIMIT = 48 * 1024 * 1024

TM_PROJ = 512
TF_DENSE = 1408
TM_MOE = 1024
TF_MOE = 512
TQ_SB = 256
HP_SB = 4
TQ_CA = 256
WIN_CA = TQ_CA + LEFT_CHUNKS * CHUNK
L_ML = 128
TC_COMB = 256


def _cparams(sem):
    return pltpu.CompilerParams(dimension_semantics=sem, vmem_limit_bytes=VMEM_LIMIT)


def _rms(x, g):
    return x * lax.rsqrt(jnp.mean(x * x, axis=-1, keepdims=True) + EPS) * g


def _dot(a, b):
    return jnp.dot(a, b, preferred_element_type=F32)


def _dot_nt(a, b):
    return lax.dot_general(a, b, (((1,), (1,)), ((), ())), preferred_element_type=F32)


def _dot_tn(a, b):
    return lax.dot_general(a, b, (((0,), (0,)), ((), ())), preferred_element_type=F32)


def _split3(x):
    hi = x.astype(BF16)
    r = x - hi.astype(F32)
    mid = r.astype(BF16)
    lo = (r - mid.astype(F32)).astype(BF16)
    return hi, mid, lo


def _log_sigmoid(x):
    return jnp.minimum(x, 0.0) - jnp.log(1.0 + jnp.exp(-jnp.abs(x)))


def _norm_proj_kernel(x_ref, g_ref, w_ref, o_ref):
    xn = _rms(x_ref[...], g_ref[...]).astype(BF16)
    o_ref[...] = _dot(xn, w_ref[...]).astype(o_ref.dtype)


def _norm_proj_gates_kernel(x_ref, g_ref, w_ref, wgt_ref, o_ref, gt_ref):
    xn = _rms(x_ref[...], g_ref[...]).astype(BF16)
    o_ref[...] = _dot(xn, w_ref[...]).astype(o_ref.dtype)
    gt_ref[...] = _dot_nt(wgt_ref[...], xn)


def norm_proj(x, g, w, wgt=None):
    t, d = x.shape
    n = w.shape[1]
    tm = TM_PROJ
    in_specs = [
        pl.BlockSpec((tm, d), lambda i: (i, 0)),
        pl.BlockSpec((1, d), lambda i: (0, 0)),
        pl.BlockSpec((d, n), lambda i: (0, 0)),
    ]
    if wgt is None:
        return pl.pallas_call(
            _norm_proj_kernel,
            out_shape=jax.ShapeDtypeStruct((t, n), BF16),
            grid=(t // tm,),
            in_specs=in_specs,
            out_specs=pl.BlockSpec((tm, n), lambda i: (i, 0)),
            compiler_params=_cparams(("parallel",)),
            name="norm_proj",
        )(x, g, w)
    ng = wgt.shape[0]
    return pl.pallas_call(
        _norm_proj_gates_kernel,
        out_shape=(jax.ShapeDtypeStruct((t, n), BF16), jax.ShapeDtypeStruct((ng, t), F32)),
        grid=(t // tm,),
        in_specs=in_specs + [pl.BlockSpec((ng, d), lambda i: (0, 0))],
        out_specs=(pl.BlockSpec((tm, n), lambda i: (i, 0)), pl.BlockSpec((ng, tm), lambda i: (0, i))),
        compiler_params=_cparams(("parallel",)),
        name="norm_proj_gates",
    )(x, g, w, wgt)


def _proj_res_kernel(a_ref, w_ref, h_ref, o_ref):
    o_ref[...] = h_ref[...] + _dot(a_ref[...], w_ref[...])


def proj_res(a, w, h):
    t, k = a.shape
    n = w.shape[1]
    tm = TM_PROJ
    return pl.pallas_call(
        _proj_res_kernel,
        out_shape=jax.ShapeDtypeStruct((t, n), F32),
        grid=(t // tm,),
        in_specs=[
            pl.BlockSpec((tm, k), lambda i: (i, 0)),
            pl.BlockSpec((k, n), lambda i: (0, 0)),
            pl.BlockSpec((tm, n), lambda i: (i, 0)),
        ],
        out_specs=pl.BlockSpec((tm, n), lambda i: (i, 0)),
        compiler_params=_cparams(("parallel",)),
        name="proj_res",
    )(a, w, h)


def _ffn_kernel(x_ref, g_ref, wg_ref, wu_ref, wd_ref, o_ref, xn_ref, acc_ref):
    f = pl.program_id(1)

    @pl.when(f == 0)
    def _():
        x = x_ref[...]
        xn_ref[...] = _rms(x, g_ref[...]).astype(BF16)
        acc_ref[...] = x

    xn = xn_ref[...]
    a = _dot(xn, wg_ref[...])
    b = _dot(xn, wu_ref[...])
    mid = (a * jax.nn.sigmoid(a) * b).astype(BF16)
    acc_ref[...] += _dot(mid, wd_ref[...])

    @pl.when(f == pl.num_programs(1) - 1)
    def _():
        o_ref[...] = acc_ref[...]


def dense_ffn(h, g, wg, wu, wd):
    t, d = h.shape
    ff = wg.shape[1]
    tm, tf = TM_PROJ, TF_DENSE
    return pl.pallas_call(
        _ffn_kernel,
        out_shape=jax.ShapeDtypeStruct((t, d), F32),
        grid=(t // tm, ff // tf),
        in_specs=[
            pl.BlockSpec((tm, d), lambda i, f: (i, 0)),
            pl.BlockSpec((1, d), lambda i, f: (0, 0)),
            pl.BlockSpec((d, tf), lambda i, f: (0, f)),
            pl.BlockSpec((d, tf), lambda i, f: (0, f)),
            pl.BlockSpec((tf, d), lambda i, f: (f, 0)),
        ],
        out_specs=pl.BlockSpec((tm, d), lambda i, f: (i, 0)),
        scratch_shapes=[pltpu.VMEM((tm, d), BF16), pltpu.VMEM((tm, d), F32)],
        compiler_params=_cparams(("parallel", "arbitrary")),
        name="dense_ffn",
    )(h, g, wg, wu, wd)


def _neg_abs(x):
    bits = lax.bitcast_convert_type(x, jnp.uint32) | jnp.uint32(0x80000000)
    return lax.bitcast_convert_type(bits, F32)


def _sb_kernel(q_ref, k_ref, v_ref, o_ref, *, tq, hp):
    i = pl.program_id(2)
    row = lax.broadcasted_iota(jnp.int32, (tq, tq), 0)
    col = lax.broadcasted_iota(jnp.int32, (tq, tq), 1)
    from_s = (row >= col).astype(BF16)
    valid = col < row
    scale = jnp.asarray(HEAD_DIM ** -0.5, BF16)
    heads = range(hp)
    sls = [slice(hh * HEAD_DIM, (hh + 1) * HEAD_DIM) for hh in heads]
    qs = [q_ref[:, sls[hh]] * scale for hh in heads]

    def blocks(js, carry, masked):
        chains = [(b, hh) for b in range(len(js)) for hh in heads]
        kks, vvs = [], []
        for j in js:
            start = pl.multiple_of(j * tq, tq)
            kks.append(k_ref[pl.ds(start, tq), :])
            vvs.append(v_ref[pl.ds(start, tq), :])
        zs = {c: _dot_nt(qs[c[1]], kks[c[0]][:, sls[c[1]]]) for c in chains}
        cats = {}
        for c in chains:
            z = zs[c]
            sp = jnp.maximum(z, 0.0) + jnp.log(1.0 + jnp.exp(_neg_abs(z)))
            if masked:
                sp = jnp.where(valid, sp, 0.0)
            cats[c] = sp.astype(BF16)
        incls = {c: _dot(cats[c], from_s) for c in chains}
        ws = {}
        for c in chains:
            w = jnp.exp(zs[c] - incls[c])
            if masked:
                w = jnp.where(valid, w, 0.0)
            ws[c] = w.astype(BF16)
        pvs = {c: _dot(ws[c], vvs[c[0]][:, sls[c[1]]]) for c in chains}
        carry = list(carry)
        for c in chains:
            acc, run = carry[c[1]]
            carry[c[1]] = (acc + jnp.exp(-run) * pvs[c], run + incls[c][:, 0:1])
        return tuple(carry)

    init = (jnp.zeros((tq, HEAD_DIM), F32), jnp.zeros((tq, 1), F32))
    carry = blocks([i], (init,) * hp, True)
    carry = lax.fori_loop(0, i // 2, lambda p, c: blocks([i - 1 - 2 * p, i - 2 - 2 * p], c, False), carry)
    carry = lax.cond(i % 2 == 1, lambda c: blocks([0], c, False), lambda c: c, carry)
    o_ref[...] = jnp.concatenate([carry[hh][0] for hh in heads], axis=1).astype(o_ref.dtype)


def sb_attention(u, batch, seq):
    t = u.shape[0]
    tq = TQ_SB
    nq = seq // tq
    hp = HP_SB
    wb = hp * HEAD_DIM
    groups = 512 // wb
    return pl.pallas_call(
        functools.partial(_sb_kernel, tq=tq, hp=hp),
        out_shape=jax.ShapeDtypeStruct((t, groups * wb), BF16),
        grid=(batch, groups, nq),
        in_specs=[
            pl.BlockSpec((tq, wb), lambda b, p, i: (b * nq + i, p)),
            pl.BlockSpec((seq, wb), lambda b, p, i: (b, groups + p)),
            pl.BlockSpec((seq, wb), lambda b, p, i: (b, 2 * groups + p)),
        ],
        out_specs=pl.BlockSpec((tq, wb), lambda b, p, i: (b * nq + i, p)),
        compiler_params=_cparams(("parallel", "parallel", "arbitrary")),
        name="sb_attention",
    )(u, u, u)


def _ca_kernel(q_ref, k_ref, v_ref, b_ref, o_ref, *, tq, win):
    i = pl.program_id(2)
    start = pl.multiple_of(jnp.maximum(i * tq - LEFT_CHUNKS * CHUNK, 0), tq)
    outs = []
    for hh in range(2):
        sl = slice(hh * HEAD_DIM, (hh + 1) * HEAD_DIM)
        q = q_ref[:, sl] * jnp.asarray(HEAD_DIM ** -0.5, BF16)
        k = k_ref[pl.ds(start, win), sl]
        v = v_ref[pl.ds(start, win), sl]
        s = _dot_nt(q, k) + b_ref[0, hh]
        m = jnp.max(s, axis=-1, keepdims=True)
        p = jnp.exp(s - m)
        den = jnp.sum(p, axis=-1, keepdims=True)
        outs.append(_dot(p.astype(BF16), v) / den)
    o_ref[...] = jnp.concatenate(outs, axis=1).astype(o_ref.dtype)


def _ca_bias_tables(rel_bias):
    pad = LEFT_CHUNKS * CHUNK
    heads = rel_bias.shape[0]
    wide = pad + WIN_CA
    n = TQ_CA + wide
    rel = pad + TQ_CA - jnp.arange(n)
    g = rel_bias[:, jnp.clip(rel, -REL_CLIP, REL_CLIP) + REL_CLIP].astype(F32)
    a = jnp.tile(g[:, ::-1], (1, TQ_CA + 1))[:, :TQ_CA * (n + 1)].reshape(heads, TQ_CA, n + 1)
    toep = a[:, :, :n][:, :, ::-1]
    bias = toep[:, :, TQ_CA:]
    qc = jnp.arange(TQ_CA)[:, None] // CHUNK
    kc = jnp.arange(wide)[None, :] // CHUNK - LEFT_CHUNKS
    ok = (kc <= qc) & (kc >= qc - LEFT_CHUNKS)
    bias = jnp.where(ok[None], bias, NEG)
    return jnp.stack([bias[:, :, s:s + WIN_CA] for s in (pad, pad - TQ_CA, 0)])


def ca_attention(u, bias_tables, batch, seq):
    t = u.shape[0]
    tq, win = TQ_CA, WIN_CA
    nq = seq // tq
    pairs = 4
    base = 3 * pairs
    return pl.pallas_call(
        functools.partial(_ca_kernel, tq=tq, win=win),
        out_shape=jax.ShapeDtypeStruct((t, pairs * 128), BF16),
        grid=(batch, pairs, nq),
        in_specs=[
            pl.BlockSpec((tq, 128), lambda b, p, i: (b * nq + i, base + p)),
            pl.BlockSpec((seq, 128), lambda b, p, i: (b, base + pairs + p)),
            pl.BlockSpec((seq, 128), lambda b, p, i: (b, base + 2 * pairs + p)),
            pl.BlockSpec((1, 2, tq, win), lambda b, p, i: (jnp.minimum(i, 2), p, 0, 0)),
        ],
        out_specs=pl.BlockSpec((tq, 128), lambda b, p, i: (b * nq + i, p)),
        compiler_params=_cparams(("parallel", "parallel", "arbitrary")),
        name="ca_attention",
    )(u, u, u, bias_tables)


def _mlstm_kernel(qk_ref, v_ref, op_ref, gt_ref, cw_ref, cb_ref, bif_ref, mg_ref, o_ref,
                  cs_ref, c_ref, n_ref, m_ref, *, L):
    c = pl.program_id(1)
    halo = 8

    @pl.when(c == 0)
    def _():
        cs_ref[0:halo, :] = jnp.zeros((halo, cs_ref.shape[1]), F32)
        c_ref[...] = jnp.zeros(c_ref.shape, F32)
        n_ref[...] = jnp.zeros(n_ref.shape, F32)
        m_ref[...] = jnp.zeros(m_ref.shape, F32)

    cs_ref[halo:halo + L, :] = qk_ref[...].astype(F32)
    y = cb_ref[...]
    for j in range(CONV_K):
        off = halo - (CONV_K - 1) + j
        y = y + cw_ref[j:j + 1, :] * cs_ref[off:off + L, :]
    cs_ref[0:halo, :] = cs_ref[L:L + halo, :]
    y = y * jax.nn.sigmoid(y)

    pre = GATE_SOFTCAP * jnp.tanh((gt_ref[...] + bif_ref[...]) / GATE_SOFTCAP)
    log_i = pre[0:ML_HEADS]
    log_f = _log_sigmoid(pre[ML_HEADS:2 * ML_HEADS])
    row = lax.broadcasted_iota(jnp.int32, (L, L), 0)
    col = lax.broadcasted_iota(jnp.int32, (L, L), 1)
    upper = (row <= col).astype(BF16)
    eye = (row == col).astype(BF16)
    causal = row >= col
    bcum = sum(_dot(part, upper) for part in _split3(log_f))
    xr = jnp.concatenate([log_i, bcum], axis=0)
    xc = sum(_dot_nt(eye, part) for part in _split3(xr))

    qw = ML_HEADS * ML_QK_DIM
    for h in range(ML_HEADS):
        q = y[:, h * ML_QK_DIM:(h + 1) * ML_QK_DIM]
        k = y[:, qw + h * ML_QK_DIM:qw + (h + 1) * ML_QK_DIM] * (ML_QK_DIM ** -0.5)
        qb = q.astype(BF16)
        vsl = slice(h * ML_V_DIM, (h + 1) * ML_V_DIM)
        v = v_ref[:, vsl]
        bc = xc[:, ML_HEADS + h:ML_HEADS + h + 1]
        lic = xc[:, h:h + 1]
        r_row = xr[h:h + 1] - xr[ML_HEADS + h:ML_HEADS + h + 1]
        m_prev = m_ref[h:h + 1, 0:1]
        dm = jnp.where(causal, bc + r_row, NEG)
        inter = bc + m_prev
        m_t = jnp.maximum(inter, jnp.max(dm, axis=-1, keepdims=True))
        s = _dot_nt(qb, k.astype(BF16)) * jnp.exp(dm - m_t)
        isc = jnp.exp(inter - m_t)
        num = _dot(s.astype(BF16), v) + isc * _dot(qb, c_ref[h].astype(BF16))
        den = (jnp.sum(s, axis=-1, keepdims=True)
               + isc * jnp.sum(q * n_ref[h:h + 1, :], axis=-1, keepdims=True))
        hv = num / jnp.maximum(jnp.abs(den), jnp.exp(-m_t))
        hn = hv * lax.rsqrt(jnp.mean(hv * hv, axis=-1, keepdims=True) + EPS) * mg_ref[:, vsl]
        o_ref[:, vsl] = (jax.nn.sigmoid(op_ref[:, vsl].astype(F32)) * hn).astype(o_ref.dtype)

        g = xr[ML_HEADS + h:ML_HEADS + h + 1, L - 1:L]
        m_new = jnp.maximum(g + m_prev, jnp.max(g + r_row, axis=-1, keepdims=True))
        kw = k * jnp.exp(g + (lic - bc) - m_new)
        carry_scale = jnp.exp(g + m_prev - m_new)
        c_ref[h] = carry_scale * c_ref[h] + _dot_tn(kw.astype(BF16), v)
        n_ref[h:h + 1, :] = carry_scale * n_ref[h:h + 1, :] + jnp.sum(kw, axis=0, keepdims=True)
        m_ref[h:h + 1, :] = jnp.broadcast_to(m_new, (1, m_ref.shape[1]))


def mlstm_core(u, gt, conv_w, conv_b, b_if, mh_g, batch, seq):
    t = u.shape[0]
    L = L_ML
    nc = seq // L
    w = D_MODEL
    return pl.pallas_call(
        functools.partial(_mlstm_kernel, L=L),
        out_shape=jax.ShapeDtypeStruct((t, w), BF16),
        grid=(batch, nc),
        in_specs=[
            pl.BlockSpec((L, w), lambda b, c: (b * nc + c, 0)),
            pl.BlockSpec((L, w), lambda b, c: (b * nc + c, 1)),
            pl.BlockSpec((L, w), lambda b, c: (b * nc + c, 2)),
            pl.BlockSpec((2 * ML_HEADS, L), lambda b, c: (0, b * nc + c)),
            pl.BlockSpec((CONV_K, w), lambda b, c: (0, 0)),
            pl.BlockSpec((1, w), lambda b, c: (0, 0)),
            pl.BlockSpec((2 * ML_HEADS, 1), lambda b, c: (0, 0)),
            pl.BlockSpec((1, w), lambda b, c: (0, 0)),
        ],
        out_specs=pl.BlockSpec((L, w), lambda b, c: (b * nc + c, 0)),
        scratch_shapes=[
            pltpu.VMEM((L + 16, w), F32),
            pltpu.VMEM((ML_HEADS, ML_QK_DIM, ML_V_DIM), F32),
            pltpu.VMEM((8, ML_QK_DIM), F32),
            pltpu.VMEM((8, 128), F32),
        ],
        compiler_params=_cparams(("parallel", "arbitrary")),
        name="mlstm_core",
    )(u, u, u, gt, conv_w, conv_b, b_if, mh_g)


def _router_kernel(x_ref, g_ref, wrt_ref, idx_ref, wt_ref):
    xn = _rms(x_ref[...], g_ref[...]).astype(BF16)
    lg = _dot_nt(wrt_ref[...], xn)
    eid = lax.broadcasted_iota(jnp.int32, lg.shape, 0)
    m1 = jnp.max(lg, axis=0, keepdims=True)
    i1 = jnp.min(jnp.where(lg == m1, eid, N_EXPERTS), axis=0, keepdims=True)
    lg2 = jnp.where(eid == i1, -jnp.inf, lg)
    m2 = jnp.max(lg2, axis=0, keepdims=True)
    i2 = jnp.min(jnp.where(lg2 == m2, eid, N_EXPERTS), axis=0, keepdims=True)
    e = jnp.exp(m2 - m1)
    idx_ref[...] = jnp.concatenate([i1, i2], axis=0)
    wt_ref[...] = jnp.concatenate([1.0 / (1.0 + e), e / (1.0 + e)], axis=0)


def router(h, g, wrt):
    t, d = h.shape
    tm = TM_PROJ
    return pl.pallas_call(
        _router_kernel,
        out_shape=(jax.ShapeDtypeStruct((2, t), jnp.int32), jax.ShapeDtypeStruct((2, t), F32)),
        grid=(t // tm,),
        in_specs=[
            pl.BlockSpec((tm, d), lambda i: (i, 0)),
            pl.BlockSpec((1, d), lambda i: (0, 0)),
            pl.BlockSpec((N_EXPERTS, d), lambda i: (0, 0)),
        ],
        out_specs=(pl.BlockSpec((2, tm), lambda i: (0, i)), pl.BlockSpec((2, tm), lambda i: (0, i))),
        compiler_params=_cparams(("parallel",)),
        name="router",
    )(h, g, wrt)


def _row_gather_start(src_hbm, idx_smem, islot, dst, sem, n_rows):
    def body(r, carry):
        tok = idx_smem[islot, r]
        pltpu.make_async_copy(src_hbm.at[pl.ds(tok, 1), :], dst.at[pl.ds(r, 1), :], sem).start()
        return carry
    lax.fori_loop(0, n_rows, body, 0, unroll=8)


def _row_gather_wait(src_hbm, dst, sem, n_rows):
    pltpu.make_async_copy(src_hbm.at[pl.ds(0, n_rows), :], dst, sem).wait()


def _moe_ffn_kernel(te_ref, tv_ref, src_ref, h_ref, g_ref, wg_ref, wu_ref, wd_ref, o_ref,
                    idx_ref, rows_ref, xn_ref, acc_ref, isem, rsem, *, tm, n_tiles):
    i = pl.program_id(0)
    f = pl.program_id(1)
    slot = i % 2

    def idx_copy(tile, s):
        return pltpu.make_async_copy(src_ref.at[tile], idx_ref.at[s], isem.at[s])

    @pl.when(f == 0)
    def _():
        @pl.when(i == 0)
        def _():
            idx_copy(0, 0).start()
            idx_copy(0, 0).wait()
            _row_gather_start(h_ref, idx_ref, 0, rows_ref.at[0], rsem.at[0], tm)
            if n_tiles > 1:
                idx_copy(1, 1).start()

        _row_gather_wait(h_ref, rows_ref.at[slot], rsem.at[slot], tm)

        @pl.when(i + 1 < n_tiles)
        def _():
            idx_copy(i + 1, 1 - slot).wait()
            _row_gather_start(h_ref, idx_ref, 1 - slot, rows_ref.at[1 - slot], rsem.at[1 - slot], tm)

        @pl.when(i + 2 < n_tiles)
        def _():
            idx_copy(i + 2, slot).start()

        xn_ref[...] = _rms(rows_ref[slot], g_ref[...]).astype(BF16)
        acc_ref[...] = jnp.zeros(acc_ref.shape, F32)

    @pl.when(tv_ref[i] > 0)
    def _():
        xn = xn_ref[...]
        a = _dot(xn, wg_ref[0])
        b = _dot(xn, wu_ref[0])
        mid = (a * jax.nn.sigmoid(a) * b).astype(BF16)
        acc_ref[...] += _dot(mid, wd_ref[0])

    @pl.when(f == pl.num_programs(1) - 1)
    def _():
        o_ref[...] = acc_ref[...]


def moe_ffn(h, g, wg, wu, wd, tile_expert, tile_valid, src_rows):
    t, d = h.shape
    ff = wg.shape[2]
    tm, tf = TM_MOE, TF_MOE
    n_tiles = src_rows.shape[0]
    nf = ff // tf
    grid_spec = pltpu.PrefetchScalarGridSpec(
        num_scalar_prefetch=2,
        grid=(n_tiles, nf),
        in_specs=[
            pl.BlockSpec(memory_space=pl.ANY),
            pl.BlockSpec(memory_space=pl.ANY),
            pl.BlockSpec((1, d), lambda i, f, te, tv: (0, 0)),
            pl.BlockSpec((1, d, tf), lambda i, f, te, tv: (te[i], 0, f)),
            pl.BlockSpec((1, d, tf), lambda i, f, te, tv: (te[i], 0, f)),
            pl.BlockSpec((1, tf, d), lambda i, f, te, tv: (te[i], f, 0)),
        ],
        out_specs=pl.BlockSpec((tm, d), lambda i, f, te, tv: (i, 0)),
        scratch_shapes=[
            pltpu.SMEM((2, tm), jnp.int32),
            pltpu.VMEM((2, tm, d), F32),
            pltpu.VMEM((tm, d), BF16),
            pltpu.VMEM((tm, d), F32),
            pltpu.SemaphoreType.DMA((2,)),
            pltpu.SemaphoreType.DMA((2,)),
        ],
    )
    return pl.pallas_call(
        functools.partial(_moe_ffn_kernel, tm=tm, n_tiles=n_tiles),
        out_shape=jax.ShapeDtypeStruct((n_tiles * tm, d), F32),
        grid_spec=grid_spec,
        compiler_params=_cparams(("arbitrary", "arbitrary")),
        name="moe_ffn",
    )(tile_expert, tile_valid, src_rows, h, g, wg, wu, wd)


def _combine_kernel(pos_ref, ys_ref, h_ref, wt_ref, g_ref, o_ref, idx_ref, rows_ref, isem, rsem,
                    *, tc, n_tiles, final_norm):
    i = pl.program_id(0)
    slot = i % 2
    n_rows = 2 * tc

    def idx_copy(tile, s):
        return pltpu.make_async_copy(pos_ref.at[tile], idx_ref.at[s], isem.at[s])

    @pl.when(i == 0)
    def _():
        idx_copy(0, 0).start()
        idx_copy(0, 0).wait()
        _row_gather_start(ys_ref, idx_ref, 0, rows_ref.at[0], rsem.at[0], n_rows)
        if n_tiles > 1:
            idx_copy(1, 1).start()

    _row_gather_wait(ys_ref, rows_ref.at[slot], rsem.at[slot], n_rows)

    @pl.when(i + 1 < n_tiles)
    def _():
        idx_copy(i + 1, 1 - slot).wait()
        _row_gather_start(ys_ref, idx_ref, 1 - slot, rows_ref.at[1 - slot], rsem.at[1 - slot], n_rows)

    @pl.when(i + 2 < n_tiles)
    def _():
        idx_copy(i + 2, slot).start()

    wt = wt_ref[...]
    y = (h_ref[...] + wt[:, 0:1] * rows_ref[slot, 0:tc, :] + wt[:, 1:2] * rows_ref[slot, tc:n_rows, :])
    if final_norm:
        y = _rms(y, g_ref[...])
    o_ref[...] = y


def moe_combine(ys, h, pos_tiles, wt, g, final_norm):
    t, d = h.shape
    tc = TC_COMB
    n_tiles = t // tc
    return pl.pallas_call(
        functools.partial(_combine_kernel, tc=tc, n_tiles=n_tiles, final_norm=final_norm),
        out_shape=jax.ShapeDtypeStruct((t, d), F32),
        grid=(n_tiles,),
        in_specs=[
            pl.BlockSpec(memory_space=pl.ANY),
            pl.BlockSpec(memory_space=pl.ANY),
            pl.BlockSpec((tc, d), lambda i: (i, 0)),
            pl.BlockSpec((tc, 2), lambda i: (i, 0)),
            pl.BlockSpec((1, d), lambda i: (0, 0)),
        ],
        out_specs=pl.BlockSpec((tc, d), lambda i: (i, 0)),
        scratch_shapes=[
            pltpu.SMEM((2, 2 * tc), jnp.int32),
            pltpu.VMEM((2, 2 * tc, d), F32),
            pltpu.SemaphoreType.DMA((2,)),
            pltpu.SemaphoreType.DMA((2,)),
        ],
        compiler_params=_cparams(("arbitrary",)),
        name="moe_combine",
    )(pos_tiles, ys, h, wt, g)


def _routing_tables(idx, t):
    tm = TM_MOE
    n_tiles = (2 * t) // tm + N_EXPERTS
    e_flat = idx.reshape(2 * t)
    onehot = (e_flat[:, None] == jnp.arange(N_EXPERTS, dtype=jnp.int32)[None, :]).astype(jnp.int32)
    csum = jnp.cumsum(onehot, axis=0)
    counts = csum[-1]
    rank = jnp.sum(csum * onehot, axis=1) - 1
    tiles_per = (counts + tm - 1) // tm
    tile_end = jnp.cumsum(tiles_per)
    offsets = (tile_end - tiles_per) * tm
    pos = offsets[e_flat] + rank
    tok = jnp.tile(jnp.arange(t, dtype=jnp.int32), 2)
    src = jnp.zeros((n_tiles * tm,), jnp.int32).at[pos].set(tok)
    tile_ids = jnp.arange(n_tiles, dtype=jnp.int32)
    tile_expert = jnp.minimum(jnp.sum(tile_ids[:, None] >= tile_end[None, :], axis=1), N_EXPERTS - 1)
    tile_valid = (tile_ids < tile_end[-1]).astype(jnp.int32)
    return (tile_expert.astype(jnp.int32), tile_valid, src.reshape(n_tiles, tm),
            pos.astype(jnp.int32).reshape(2, t))


def moe_block(h, g, wrt, wg, wu, wd, final_g=None):
    t, d = h.shape
    idx, wts = router(h, g, wrt)
    tile_expert, tile_valid, src, pos = _routing_tables(idx, t)
    ys = moe_ffn(h, g, wg, wu, wd, tile_expert, tile_valid, src)
    tc = TC_COMB
    pos_tiles = pos.reshape(2, t // tc, tc).transpose(1, 0, 2).reshape(t // tc, 2 * tc)
    gn = final_g if final_g is not None else g
    return moe_combine(ys, h, pos_tiles, wts.T, gn, final_g is not None)


def kernel(x, mix_norm_even, w_in_attn, rel_bias, w_out_attn, ffn_norm_even, w_dense_gate, w_dense_up, w_dense_down, mix_norm_odd, w_in_mlstm, conv_w, conv_b, b_igate, b_fgate, mh_norm_g, w_out_mlstm, ffn_norm_odd, w_router, w_exp_gate, w_exp_up, w_exp_down, final_norm_g):
    batch, seq, d = x.shape
    t = batch * seq
    depth = mix_norm_even.shape[0] + mix_norm_odd.shape[0]
    h = x.reshape(t, d)
    main_w = 3 * D_MODEL
    for layer in range(depth):
        j = layer // 2
        if layer % 2 == 0:
            u = norm_proj(h, mix_norm_even[j][None], w_in_attn[j].astype(BF16))
            oa = sb_attention(u, batch, seq)
            ob = ca_attention(u, _ca_bias_tables(rel_bias[j]), batch, seq)
            o = jnp.concatenate([oa, ob], axis=1)
            h = proj_res(o, w_out_attn[j].astype(BF16), h)
            h = dense_ffn(h, ffn_norm_even[j][None], w_dense_gate[j].astype(BF16),
                          w_dense_up[j].astype(BF16), w_dense_down[j].astype(BF16))
        else:
            w_in = w_in_mlstm[j]
            u, gt = norm_proj(h, mix_norm_odd[j][None], w_in[:, :main_w].astype(BF16),
                              w_in[:, main_w:].T.astype(BF16))
            b_if = jnp.concatenate([b_igate[j], b_fgate[j]])[:, None]
            hm = mlstm_core(u, gt, conv_w[j], conv_b[j][None], b_if, mh_norm_g[j][None], batch, seq)
            h = proj_res(hm, w_out_mlstm[j].astype(BF16), h)
            last = layer == depth - 1
            h = moe_block(h, ffn_norm_odd[j][None], w_router[j].T.astype(BF16),
                          w_exp_gate[j].astype(BF16), w_exp_up[j].astype(BF16),
                          w_exp_down[j].astype(BF16), final_norm_g[None] if last else None)
    if depth % 2 == 1:
        raise NotImplementedError("final norm is fused into the last MoE combine")
    return h.reshape(batch, seq, d)
```
